```python
import jax, jax.numpy as jnp
from jax import lax
import numpy as np

D_MODEL = 2048
BATCH = 2
SEQ = 8192
DEPTH = 1

MIX_WIDTH = D_MODEL
HEAD_DIM = 64
RWKV_WIDTH = MIX_WIDTH // 2
FOX_WIDTH = MIX_WIDTH - RWKV_WIDTH
RWKV_HEADS = RWKV_WIDTH // HEAD_DIM
FOX_HEADS = FOX_WIDTH // HEAD_DIM
DECAY_LORA = 64
ICLR_LORA = 64
GATE_LORA = 128
RWKV_COLS = 3 * RWKV_WIDTH + DECAY_LORA + ICLR_LORA + GATE_LORA
FOX_COLS = 3 * FOX_WIDTH + FOX_HEADS
IN_COLS = RWKV_COLS + FOX_COLS
Q_BLOCK = 128
RMS_EPS = 1e-6
LNX_EPS = 64e-5
PEER_HEADS = 8
PEER_NKEYS = 128
PEER_EXPERTS = PEER_NKEYS * PEER_NKEYS
PEER_QDIM = 128
PEER_HALF = PEER_QDIM // 2
PEER_TOPK = 16
PEER_TOK_BLOCK = 128

kernel_name = "hymba_rwkv7_fox_peer_layer"


def rms_norm(x, g):
    xf = x.astype(jnp.float32)
    xf = xf * lax.rsqrt(jnp.mean(xf * xf, axis=-1, keepdims=True) + RMS_EPS)
    return (xf * g.astype(jnp.float32)).astype(x.dtype)


def rwkv7_scan(r, decay, k, v, a, b):
    B, S, H, N = r.shape

    def step(state, xs):
        r_t, w_t, k_t, v_t, a_t, b_t = xs
        sa = jnp.einsum('bhvk,bhk->bhv', state, a_t)
        state = (state * w_t[:, :, None, :]
                 + sa[..., None] * b_t[:, :, None, :]
                 + v_t[..., None] * k_t[:, :, None, :])
        y = jnp.einsum('bhvk,bhk->bhv', state, r_t)
        return state, y

    xs = tuple(jnp.moveaxis(t.astype(jnp.float32), 1, 0) for t in (r, decay, k, v, a, b))
    state0 = jnp.zeros((B, H, N, N), jnp.float32)
    _, ys = lax.scan(step, state0, xs)
    return jnp.moveaxis(ys, 0, 1)


def rwkv7_group(p, mu_shift, w_decay_up, w0, w_iclr_up, a0, w_gate_up, k_k, k_a, r_k, lnx_g, lnx_b):
    B, S, _ = p.shape
    H, N = RWKV_HEADS, HEAD_DIM
    pf = p.astype(jnp.float32)
    prev = jnp.pad(pf, ((0, 0), (1, 0), (0, 0)))[:, :-1]
    pf = pf + mu_shift.astype(jnp.float32) * (prev - pf)
    o = np.cumsum([RWKV_WIDTH, RWKV_WIDTH, RWKV_WIDTH, DECAY_LORA, ICLR_LORA])
    r, k, v, wd, ad, gd = jnp.split(pf, o.tolist(), axis=-1)
    w_raw = -jax.nn.softplus(-(w0 + jnp.tanh(wd) @ w_decay_up)) - 0.5
    decay = jnp.exp(-jnp.exp(w_raw))
    a = jax.nn.sigmoid(a0 + ad @ w_iclr_up)
    g = jax.nn.sigmoid(gd) @ w_gate_up
    kk = (k * k_k).reshape(B, S, H, N)
    kk = kk / jnp.maximum(jnp.linalg.norm(kk, axis=-1, keepdims=True), 1e-12)
    k = k * (1.0 + (a - 1.0) * k_a)
    hs = lambda t: t.reshape(B, S, H, N)
    r_h, k_h, v_h, a_h = hs(r), hs(k), hs(v), hs(a)
    y = rwkv7_scan(r_h, hs(decay), k_h, v_h, -kk, kk * a_h)
    mean = jnp.mean(y, axis=-1, keepdims=True)
    var = jnp.mean(jnp.square(y - mean), axis=-1, keepdims=True)
    y = ((y - mean) * lax.rsqrt(var + LNX_EPS)).reshape(B, S, RWKV_WIDTH) * lnx_g + lnx_b
    bonus = jnp.sum(r_h * k_h * r_k, axis=-1, keepdims=True) * v_h
    y = (y + bonus.reshape(B, S, RWKV_WIDTH)) * g
    return y.astype(p.dtype)


def fox_group(p, fox_b_f, q_norm_g, k_norm_g):
    B, S, _ = p.shape
    H, D = FOX_HEADS, HEAD_DIM
    NB = S // Q_BLOCK
    o = np.cumsum([FOX_WIDTH, FOX_WIDTH, FOX_WIDTH])
    q, k, v, f_logit = jnp.split(p, o.tolist(), axis=-1)
    q = rms_norm(q.reshape(B, S, H, D), q_norm_g).transpose(0, 2, 1, 3)
    k = rms_norm(k.reshape(B, S, H, D), k_norm_g).transpose(0, 2, 1, 3)
    v = v.reshape(B, S, H, D).transpose(0, 2, 1, 3)
    log_f = jax.nn.log_sigmoid(f_logit.astype(jnp.float32) + fox_b_f.astype(jnp.float32))
    c = jnp.cumsum(log_f, axis=1).transpose(0, 2, 1)
    scale = HEAD_DIM ** -0.5
    q_blocks = jnp.moveaxis(q.reshape(B, H, NB, Q_BLOCK, D), 2, 0)
    c_blocks = jnp.moveaxis(c.reshape(B, H, NB, Q_BLOCK), 2, 0)
    starts = jnp.arange(NB, dtype=jnp.int32) * Q_BLOCK
    key_pos = jnp.arange(S, dtype=jnp.int32)

    def attend(args):
        qb, cb, start = args
        s = jnp.einsum('bhqd,bhkd->bhqk', qb, k).astype(jnp.float32) * scale
        s = s + cb[..., :, None] - c[:, :, None, :]
        q_pos = start + jnp.arange(Q_BLOCK, dtype=jnp.int32)
        mask = key_pos[None, :] <= q_pos[:, None]
        s = jnp.where(mask, s, -jnp.inf)
        prob = jax.nn.softmax(s, axis=-1)
        return jnp.einsum('bhqk,bhkd->bhqd', prob.astype(v.dtype), v)

    out = lax.map(attend, (q_blocks, c_blocks, starts))
    return out.transpose(1, 0, 3, 2, 4).reshape(B, S, FOX_WIDTH).astype(p.dtype)


def peer(xn, w_q, sub_keys, u_tab, v_tab):
    B, S, D = xn.shape
    T = B * S
    xt = xn.reshape(T // PEER_TOK_BLOCK, PEER_TOK_BLOCK, D)

    def block(xb):
        q = (xb @ w_q).reshape(PEER_TOK_BLOCK, PEER_HEADS, 2, PEER_HALF).astype(jnp.float32)
        s = jnp.einsum('thpd,hpnd->thpn', q, sub_keys.astype(jnp.float32))
        s1, i1 = lax.top_k(s[:, :, 0], PEER_TOPK)
        s2, i2 = lax.top_k(s[:, :, 1], PEER_TOPK)
        cand = (s1[..., :, None] + s2[..., None, :]).reshape(PEER_TOK_BLOCK, PEER_HEADS, PEER_TOPK * PEER_TOPK)
        cidx = (i1[..., :, None] * PEER_NKEYS + i2[..., None, :]).reshape(PEER_TOK_BLOCK, PEER_HEADS, PEER_TOPK * PEER_TOPK)
        top, pos = lax.top_k(cand, PEER_TOPK)
        idx = jnp.take_along_axis(cidx, pos, axis=-1)
        gate = jax.nn.softmax(top, axis=-1)
        u = u_tab[idx]
        h = jax.nn.gelu(jnp.einsum('thkd,td->thk', u, xb), approximate=False)
        vv = v_tab[idx]
        return jnp.einsum('thk,thkd->td', gate.astype(h.dtype) * h, vv)

    return lax.map(block, xt).reshape(B, S, D).astype(xn.dtype)


def setup_inputs(seed: int = 0) -> dict:
    key = jax.random.key(seed)
    ks = jax.random.split(key, 24)
    L = DEPTH
    f32 = jnp.float32
    nrm = lambda k, shape, sc: jax.random.normal(k, shape, f32) * sc
    return {
        "x": nrm(ks[0], (BATCH, SEQ, D_MODEL), 1.0),
        "norm1_g": 1.0 + nrm(ks[1], (L, D_MODEL), 0.02),
        "w_in": nrm(ks[2], (L, D_MODEL, IN_COLS), D_MODEL ** -0.5),
        "mu_shift": jax.random.uniform(ks[3], (L, RWKV_COLS), f32, 0.0, 1.0),
        "w_decay_up": nrm(ks[4], (L, DECAY_LORA, RWKV_WIDTH), 0.5 * DECAY_LORA ** -0.5),
        "w0": jax.random.uniform(ks[5], (L, RWKV_WIDTH), f32, -5.0, 0.0),
        "w_iclr_up": nrm(ks[6], (L, ICLR_LORA, RWKV_WIDTH), 0.5 * ICLR_LORA ** -0.5),
        "a0": nrm(ks[7], (L, RWKV_WIDTH), 0.5),
        "w_gate_up": nrm(ks[8], (L, GATE_LORA, RWKV_WIDTH), GATE_LORA ** -0.5),
        "k_k": 0.85 + nrm(ks[9], (L, RWKV_WIDTH), 0.05),
        "k_a": 1.0 + nrm(ks[10], (L, RWKV_WIDTH), 0.05),
        "r_k": nrm(ks[11], (L, RWKV_HEADS, HEAD_DIM), 0.1),
        "lnx_g": 1.0 + nrm(ks[12], (L, RWKV_WIDTH), 0.02),
        "lnx_b": nrm(ks[13], (L, RWKV_WIDTH), 0.02),
        "fox_b_f": jax.random.uniform(ks[14], (L, FOX_HEADS), f32, 1.0, 5.0),
        "q_norm_g": 1.0 + nrm(ks[15], (L, HEAD_DIM), 0.02),
        "k_norm_g": 1.0 + nrm(ks[16], (L, HEAD_DIM), 0.02),
        "w_out": nrm(ks[17], (L, MIX_WIDTH, D_MODEL), MIX_WIDTH ** -0.5),
        "norm2_g": 1.0 + nrm(ks[18], (L, D_MODEL), 0.02),
        "peer_w_q": nrm(ks[19], (L, D_MODEL, PEER_HEADS * PEER_QDIM), D_MODEL ** -0.5),
        "peer_sub_keys": nrm(ks[20], (L, PEER_HEADS, 2, PEER_NKEYS, PEER_HALF), PEER_HALF ** -0.5),
        "peer_u": nrm(ks[21], (L, PEER_EXPERTS, D_MODEL), D_MODEL ** -0.5),
        "peer_v": nrm(ks[22], (L, PEER_EXPERTS, D_MODEL), D_MODEL ** -0.5),
    }


def reference(x, norm1_g, w_in, mu_shift, w_decay_up, w0, w_iclr_up, a0, w_gate_up,
              k_k, k_a, r_k, lnx_g, lnx_b, fox_b_f, q_norm_g, k_norm_g, w_out,
              norm2_g, peer_w_q, peer_sub_keys, peer_u, peer_v):
    h = x
    for l in range(DEPTH):
        xn = rms_norm(h, norm1_g[l])
        proj = xn @ w_in[l]
        p_rwkv, p_fox = proj[..., :RWKV_COLS], proj[..., RWKV_COLS:]
        y_rwkv = rwkv7_group(p_rwkv, mu_shift[l], w_decay_up[l], w0[l], w_iclr_up[l], a0[l],
                             w_gate_up[l], k_k[l], k_a[l], r_k[l], lnx_g[l], lnx_b[l])
        y_fox = fox_group(p_fox, fox_b_f[l], q_norm_g[l], k_norm_g[l])
        mixed = jnp.concatenate([y_rwkv.astype(h.dtype), y_fox.astype(h.dtype)], axis=-1)
        h = h + (mixed @ w_out[l]).astype(h.dtype)
        h = h + peer(rms_norm(h, norm2_g[l]), peer_w_q[l], peer_sub_keys[l], peer_u[l], peer_v[l])
    return h
```

```python
import functools

import jax
import jax.numpy as jnp
import numpy as np
from jax import lax
from jax.experimental import pallas as pl
from jax.experimental.pallas import tpu as pltpu

F32 = jnp.float32
BF16 = jnp.bfloat16

HEAD_DIM = 64
LANES = 128
RMS_EPS = 1e-6
LNX_EPS = 64e-5
PEER_TOPK = 16
NEG = -1e30
VMEM_LIMIT = 56 * 1024 * 1024

NN = (((1,), (0,)), ((), ()))
NT = (((1,), (1,)), ((), ()))
TN = (((0,), (0,)), ((), ()))


def _params(*sem):
    return pltpu.CompilerParams(dimension_semantics=sem, vmem_limit_bytes=VMEM_LIMIT)


def _split(x, n):
    if x.dtype == BF16:
        return [x]
    pieces, rem = [], x
    for i in range(n):
        p = rem.astype(BF16)
        pieces.append(p)
        if i + 1 < n:
            rem = rem - p.astype(F32)
    return pieces


def _mm(a, b, dims=NN, pa=1, pb=1):
    A, B = _split(a, pa), _split(b, pb)
    order = max(len(A), len(B))
    out = None
    for i in reversed(range(len(A))):
        for j in reversed(range(len(B))):
            if i + j >= order:
                continue
            t = lax.dot_general(A[i], B[j], dims, preferred_element_type=F32)
            out = t if out is None else out + t
    return out


def _inproj_body(x_ref, g_ref, w_ref, o_ref, xn_ref):
    @pl.when(pl.program_id(1) == 0)
    def _():
        xf = x_ref[...]
        ms = jnp.mean(xf * xf, axis=-1, keepdims=True)
        xn_ref[...] = (xf * lax.rsqrt(ms + RMS_EPS) * g_ref[...]).astype(BF16)

    o_ref[...] = jnp.dot(xn_ref[...], w_ref[...], preferred_element_type=F32)


def _inproj(x2, g, w_pad):
    T, D = x2.shape
    NC = w_pad.shape[1]
    tm = min(512, T)
    tn = NC // 3
    return pl.pallas_call(
        _inproj_body,
        grid=(T // tm, NC // tn),
        in_specs=[pl.BlockSpec((tm, D), lambda i, j: (i, 0)),
                  pl.BlockSpec((1, D), lambda i, j: (0, 0)),
                  pl.BlockSpec((D, tn), lambda i, j: (0, j))],
        out_specs=pl.BlockSpec((tm, tn), lambda i, j: (i, j)),
        out_shape=jax.ShapeDtypeStruct((T, NC), F32),
        scratch_shapes=[pltpu.VMEM((tm, D), BF16)],
        compiler_params=_params("parallel", "arbitrary"),
        name="inproj",
    )(x2, g.reshape(1, D), w_pad)


def _softplus(z):
    return jnp.maximum(z, 0.0) + jnp.log1p(jnp.exp(-jnp.abs(z)))


def _rwkv_prep_body(r_ref, rp_ref, k_ref, kp_ref, v_ref, vp_ref, lo_ref, lop_ref, gd_ref, gdp_ref,
                    mur_ref, muk_ref, muv_ref, mulo_ref, mugd_ref,
                    wdec_ref, w0_ref, wicl_ref, a0_ref, wgate_ref, kk_ref, ka_ref, gsum_ref,
                    ro_ref, lwo_ref, ko_ref, vo_ref, ao_ref, bo_ref, go_ref, *, tiles_per_seq):
    first = (pl.program_id(0) % tiles_per_seq) == 0

    def shift(cur_ref, prev_ref, mu_ref):
        cur = cur_ref[...]
        rolled = pltpu.roll(cur, 1, 0)
        prow = jnp.where(first, 0.0, prev_ref[7:8, :])
        row = lax.broadcasted_iota(jnp.int32, cur.shape, 0)
        prev = jnp.where(row == 0, prow, rolled)
        return cur + mu_ref[...] * (prev - cur)

    r = shift(r_ref, rp_ref, mur_ref)
    k = shift(k_ref, kp_ref, muk_ref)
    v = shift(v_ref, vp_ref, muv_ref)
    lo = shift(lo_ref, lop_ref, mulo_ref)
    gd = shift(gd_ref, gdp_ref, mugd_ref)

    w_raw = -_softplus(-(w0_ref[...] + _mm(jnp.tanh(lo), wdec_ref[...], pa=2, pb=2))) - 0.5
    lwo_ref[...] = -jnp.exp(w_raw)
    a = jax.nn.sigmoid(a0_ref[...] + _mm(lo, wicl_ref[...], pa=2, pb=2))
    go_ref[...] = _mm(jax.nn.sigmoid(gd), wgate_ref[...], pa=2, pb=2)
    kk = k * kk_ref[...]
    gsum = gsum_ref[...]
    for j in range(kk.shape[1] // LANES):
        sl = slice(j * LANES, (j + 1) * LANES)
        kkj = kk[:, sl]
        ss = _mm(kkj * kkj, gsum, pa=3, pb=1)
        kkn = kkj / jnp.maximum(jnp.sqrt(ss), 1e-12)
        ao_ref[:, sl] = -kkn
        bo_ref[:, sl] = kkn * a[:, sl]
    ro_ref[...] = r
    vo_ref[...] = v
    ko_ref[...] = k * (1.0 + (a - 1.0) * ka_ref[...])


def _rwkv_prep(proj, S, col0, mu, w_decay_up, w0, w_iclr_up, a0, w_gate_up, k_k, k_a):
    T = proj.shape[0]
    W = w0.shape[0]
    dl, il = w_decay_up.shape[0], w_iclr_up.shape[0]
    assert dl + il == LANES and w_gate_up.shape[0] == LANES and W % LANES == 0 and col0 % W == 0
    tm = min(256, S)
    cb = col0 // W
    lb = (col0 + 3 * W) // LANES
    nprev = tm // 8

    def cur(width, c):
        return pl.BlockSpec((tm, width), lambda i: (i, c))

    def prev(width, c):
        return pl.BlockSpec((8, width), lambda i: (jnp.maximum(i * nprev - 1, 0), c))

    def full(shape):
        return pl.BlockSpec(shape, lambda i: (0,) * len(shape))

    row = lambda a: a.reshape(1, -1)
    wdec = jnp.concatenate([w_decay_up, jnp.zeros((il, W), F32)], axis=0)
    wicl = jnp.concatenate([jnp.zeros((dl, W), F32), w_iclr_up], axis=0)
    gsum = jnp.asarray(np.kron(np.eye(LANES // HEAD_DIM), np.ones((HEAD_DIM, HEAD_DIM))), F32)
    o = np.cumsum([W, W, W, dl + il])
    outs = pl.pallas_call(
        functools.partial(_rwkv_prep_body, tiles_per_seq=S // tm),
        grid=(T // tm,),
        in_specs=[cur(W, cb), prev(W, cb), cur(W, cb + 1), prev(W, cb + 1), cur(W, cb + 2), prev(W, cb + 2),
                  cur(LANES, lb), prev(LANES, lb), cur(LANES, lb + 1), prev(LANES, lb + 1),
                  full((1, W)), full((1, W)), full((1, W)), full((1, LANES)), full((1, LANES)),
                  full((LANES, W)), full((1, W)), full((LANES, W)), full((1, W)), full((LANES, W)),
                  full((1, W)), full((1, W)), full((LANES, LANES))],
        out_specs=[pl.BlockSpec((tm, W), lambda i: (i, 0))] * 7,
        out_shape=[jax.ShapeDtypeStruct((T, W), F32)] * 7,
        compiler_params=_params("parallel"),
        name="rwkv_prep",
    )(proj, proj, proj, proj, proj, proj, proj, proj, proj, proj,
      row(mu[:o[0]]), row(mu[o[0]:o[1]]), row(mu[o[1]:o[2]]), row(mu[o[2]:o[3]]), row(mu[o[3]:]),
      wdec, row(w0), wicl, row(a0), w_gate_up, row(k_k), row(k_a), gsum)
    return outs


def _scan_body(r_ref, lw_ref, k_ref, v_ref, a_ref, b_ref, y_ref, s_ref, *, C, LC):
    @pl.when(pl.program_id(2) == 0)
    def _():
        s_ref[...] = jnp.zeros_like(s_ref)

    lane = lax.broadcasted_iota(jnp.int32, (1, LANES), 1)
    head_a = lane < HEAD_DIM
    m_a = head_a.astype(F32)
    m_b = 1.0 - m_a
    ri = lax.broadcasted_iota(jnp.int32, (C, C), 0)
    ci = lax.broadcasted_iota(jnp.int32, (C, C), 1)
    tri_incl = ri >= ci
    tri_strict = ri > ci
    tri_f = tri_incl.astype(F32)
    eye = (ri == ci).astype(F32)
    bi = lax.broadcasted_iota(jnp.int32, (LANES, LANES), 0) // HEAD_DIM
    bj = lax.broadcasted_iota(jnp.int32, (LANES, LANES), 1) // HEAD_DIM
    bd = (bi == bj).astype(F32)

    mm = functools.partial(_mm, pa=2, pb=2)
    sel = lambda xa, xb: jnp.where(head_a, xa, xb)

    S = s_ref[...]
    for i in range(LC):
        sl = slice(i * C, (i + 1) * C)
        r, lw, k, v, a, b = (ref[0, sl, :] for ref in (r_ref, lw_ref, k_ref, v_ref, a_ref, b_ref))
        cum = _mm(tri_f, lw, pa=1, pb=3)
        p_in = jnp.exp(cum)
        p_inv = jnp.exp(-cum)
        at = a * jnp.exp(cum - lw)
        bt = b * p_inv
        kt = k * p_inv
        rt = r * p_in
        cum_end = cum[C - 1:C, :]
        p_end = jnp.exp(cum_end)
        to_end = jnp.exp(cum_end - cum)
        bh = b * to_end
        kh = k * to_end

        def per_head(m):
            atm, rtm = at * m, rt * m
            a_ab = jnp.where(tri_strict, mm(atm, bt, NT), 0.0)
            a_ak = jnp.where(tri_strict, mm(atm, kt, NT), 0.0)
            a_rb = jnp.where(tri_incl, mm(rtm, bt, NT), 0.0)
            a_rk = jnp.where(tri_incl, mm(rtm, kt, NT), 0.0)
            tinv, pw = eye + a_ab, a_ab
            n = 2
            while n < C:
                pw = mm(pw, pw)
                tinv = tinv + mm(tinv, pw)
                n *= 2
            return a_ak, a_rb, a_rk, tinv

        ak_a, rb_a, rk_a, t_a = per_head(m_a)
        ak_b, rb_b, rk_b, t_b = per_head(m_b)
        x = mm(at, S, NT) + sel(mm(ak_a, v), mm(ak_b, v))
        u = sel(mm(t_a, x), mm(t_b, x))
        y = mm(rt, S, NT) + sel(mm(rb_a, u) + mm(rk_a, v), mm(rb_b, u) + mm(rk_b, v))
        y_ref[0, sl, :] = y
        S = S * p_end + bd * (mm(u, bh, TN) + mm(v, kh, TN))
    s_ref[...] = S


def _rwkv_scan(r, lw, k, v, a, b, B, S):
    W = r.shape[-1]
    C = min(64, S)
    LC = max(1, min(4, S // C))
    R = C * LC
    spec = pl.BlockSpec((1, R, LANES), lambda bi, h, c: (bi, c, h))
    args = [t.reshape(B, S, W) for t in (r, lw, k, v, a, b)]
    y = pl.pallas_call(
        functools.partial(_scan_body, C=C, LC=LC),
        grid=(B, W // LANES, S // R),
        in_specs=[spec] * 6,
        out_specs=spec,
        out_shape=jax.ShapeDtypeStruct((B, S, W), F32),
        scratch_shapes=[pltpu.VMEM((LANES, LANES), F32)],
        compiler_params=_params("parallel", "parallel", "arbitrary"),
        name="rwkv_scan",
    )(*args)
    return y.reshape(B * S, W)


def _rwkv_post_body(y_ref, r_ref, k_ref, v_ref, g_ref, lg_ref, lb_ref, rk_ref, gmean_ref, gsum_ref, o_ref):
    gmean, gsum = gmean_ref[...], gsum_ref[...]
    for j in range(y_ref.shape[1] // LANES):
        sl = slice(j * LANES, (j + 1) * LANES)
        y = y_ref[:, sl]
        mean = _mm(y, gmean, pa=3, pb=1)
        d = y - mean
        var = _mm(d * d, gmean, pa=3, pb=1)
        yn = d * lax.rsqrt(var + LNX_EPS) * lg_ref[:, sl] + lb_ref[:, sl]
        v = v_ref[:, sl]
        bonus = _mm(r_ref[:, sl] * k_ref[:, sl] * rk_ref[:, sl], gsum, pa=3, pb=1) * v
        o_ref[:, sl] = ((yn + bonus) * g_ref[:, sl]).astype(o_ref.dtype)


def _rwkv_post(y, r, k, v, g, lnx_g, lnx_b, r_k):
    T, W = y.shape
    tm = min(512, T)
    blk = np.kron(np.eye(LANES // HEAD_DIM), np.ones((HEAD_DIM, HEAD_DIM)))
    tile = pl.BlockSpec((tm, W), lambda i: (i, 0))
    rowspec = pl.BlockSpec((1, W), lambda i: (0, 0))
    sq = pl.BlockSpec((LANES, LANES), lambda i: (0, 0))
    return pl.pallas_call(
        _rwkv_post_body,
        grid=(T // tm,),
        in_specs=[tile] * 5 + [rowspec] * 3 + [sq] * 2,
        out_specs=tile,
        out_shape=jax.ShapeDtypeStruct((T, W), BF16),
        compiler_params=_params("parallel"),
        name="rwkv_post",
    )(y, r, k, v, g, lnx_g.reshape(1, W), lnx_b.reshape(1, W), r_k.reshape(1, W),
      jnp.asarray(blk / HEAD_DIM, F32), jnp.asarray(blk, F32))


def _fox_prep_body(q_ref, k_ref, f_ref, qg_ref, kg_ref, fb_ref, gmean_ref, qo_ref, ko_ref, c_ref, carry_ref,
                   *, tiles_per_seq, scale):
    @pl.when(pl.program_id(0) % tiles_per_seq == 0)
    def _():
        carry_ref[...] = jnp.zeros_like(carry_ref)

    gmean = gmean_ref[...]
    for j in range(q_ref.shape[1] // LANES):
        sl = slice(j * LANES, (j + 1) * LANES)
        for src, gain, dst, mul in ((q_ref, qg_ref, qo_ref, scale), (k_ref, kg_ref, ko_ref, 1.0)):
            t = src[:, sl]
            ms = _mm(t * t, gmean, pa=3, pb=1)
            dst[:, sl] = (t * lax.rsqrt(ms + RMS_EPS) * gain[...] * mul).astype(dst.dtype)

    log_f = jax.nn.log_sigmoid(f_ref[...] + fb_ref[...])
    tm = log_f.shape[0]
    ri = lax.broadcasted_iota(jnp.int32, (tm, tm), 0)
    ci = lax.broadcasted_iota(jnp.int32, (tm, tm), 1)
    c = _mm((ri >= ci).astype(F32), log_f, pa=1, pb=3) + carry_ref[...]
    c_ref[...] = c
    carry_ref[...] = c[tm - 1:tm, :]


def _fox_prep(proj, S, q_norm_g, k_norm_g, fox_b_f, W, f_col):
    T = proj.shape[0]
    tm = min(256, S)
    nh = fox_b_f.shape[0]
    gq = jnp.tile(q_norm_g, LANES // HEAD_DIM).reshape(1, LANES)
    gk = jnp.tile(k_norm_g, LANES // HEAD_DIM).reshape(1, LANES)
    fb = jnp.concatenate([fox_b_f, jnp.zeros((LANES - nh,), F32)]).reshape(1, LANES)
    blk = np.kron(np.eye(LANES // HEAD_DIM), np.ones((HEAD_DIM, HEAD_DIM))) / HEAD_DIM
    small = pl.BlockSpec((1, LANES), lambda i: (0, 0))
    return pl.pallas_call(
        functools.partial(_fox_prep_body, tiles_per_seq=S // tm, scale=HEAD_DIM ** -0.5),
        grid=(T // tm,),
        in_specs=[pl.BlockSpec((tm, W), lambda i: (i, 0)), pl.BlockSpec((tm, W), lambda i: (i, 1)),
                  pl.BlockSpec((tm, LANES), lambda i: (i, f_col // LANES)),
                  small, small, small, pl.BlockSpec((LANES, LANES), lambda i: (0, 0))],
        out_specs=[pl.BlockSpec((tm, W), lambda i: (i, 0)), pl.BlockSpec((tm, W), lambda i: (i, 0)),
                   pl.BlockSpec((tm, LANES), lambda i: (i, 0))],
        out_shape=[jax.ShapeDtypeStruct((T, W), BF16), jax.ShapeDtypeStruct((T, W), BF16),
                   jax.ShapeDtypeStruct((T, LANES), F32)],
        scratch_shapes=[pltpu.VMEM((1, LANES), F32)],
        compiler_params=_params("arbitrary"),
        name="fox_prep",
    )(proj, proj, proj, gq, gk, fb, jnp.asarray(blk, F32))


def _attn_body(q_ref, k_ref, v_ref, cq_ref, ck_ref, o_ref, m_ref, l_ref, acc_ref, *, tq, tk):
    qi, kj = pl.program_id(2), pl.program_id(3)

    @pl.when(kj == 0)
    def _():
        m_ref[...] = jnp.full_like(m_ref, NEG)
        l_ref[...] = jnp.zeros_like(l_ref)
        acc_ref[...] = jnp.zeros_like(acc_ref)

    @pl.when(kj * tk <= qi * tq + tq - 1)
    def _():
        s = lax.dot_general(q_ref[0, 0], k_ref[0, 0], NT, preferred_element_type=F32)
        s = s + cq_ref[0, 0] - ck_ref[0, 0]
        qpos = qi * tq + lax.broadcasted_iota(jnp.int32, (tq, tk), 0)
        kpos = kj * tk + lax.broadcasted_iota(jnp.int32, (tq, tk), 1)
        s = jnp.where(kpos <= qpos, s, NEG)
        m_old = m_ref[...]
        m_new = jnp.maximum(m_old, jnp.max(s, axis=-1, keepdims=True))
        p = jnp.exp(s - m_new)
        alpha = jnp.exp(m_old - m_new)
        l_ref[...] = alpha * l_ref[...] + jnp.sum(p, axis=-1, keepdims=True)
        acc_ref[...] = alpha * acc_ref[...] + jnp.dot(p.astype(BF16), v_ref[0, 0], preferred_element_type=F32)
        m_ref[...] = m_new

    @pl.when(kj == pl.num_programs(3) - 1)
    def _():
        o_ref[0, 0] = (acc_ref[...] / l_ref[...]).astype(o_ref.dtype)


def _fox_attn(q, k, v, c):
    B, H, S, D = q.shape
    tq = min(512, S)
    tk = min(512, S)
    last = lambda i: (i * tq + tq - 1) // tk
    kv_spec = pl.BlockSpec((1, 1, tk, D), lambda b, h, i, j: (b, h, jnp.minimum(j, last(i)), 0))
    return pl.pallas_call(
        functools.partial(_attn_body, tq=tq, tk=tk),
        grid=(B, H, S // tq, S // tk),
        in_specs=[pl.BlockSpec((1, 1, tq, D), lambda b, h, i, j: (b, h, i, 0)), kv_spec, kv_spec,
                  pl.BlockSpec((1, 1, tq, 1), lambda b, h, i, j: (b, h, i, 0)),
                  pl.BlockSpec((1, 1, 1, tk), lambda b, h, i, j: (b, h, 0, jnp.minimum(j, last(i))))],
        out_specs=pl.BlockSpec((1, 1, tq, D), lambda b, h, i, j: (b, h, i, 0)),
        out_shape=jax.ShapeDtypeStruct((B, H, S, D), BF16),
        scratch_shapes=[pltpu.VMEM((tq, 1), F32), pltpu.VMEM((tq, 1), F32), pltpu.VMEM((tq, D), F32)],
        compiler_params=_params("parallel", "parallel", "parallel", "arbitrary"),
        name="fox_attn",
    )(q, k, v, c.reshape(B, H, S, 1), c.reshape(B, H, 1, S))


def _outproj_body(mr_ref, mf_ref, x_ref, wr_ref, wf_ref, g_ref, h_ref, xn_ref):
    h = (x_ref[...] + jnp.dot(mr_ref[...], wr_ref[...], preferred_element_type=F32)
         + jnp.dot(mf_ref[...], wf_ref[...], preferred_element_type=F32))
    h_ref[...] = h
    ms = jnp.mean(h * h, axis=-1, keepdims=True)
    xn_ref[...] = h * lax.rsqrt(ms + RMS_EPS) * g_ref[...]


def _outproj(mix_r, mix_f, x2, w_out, g2):
    T, D = x2.shape
    W = mix_r.shape[1]
    tm = min(256, T)
    wr, wf = w_out[:W].astype(BF16), w_out[W:].astype(BF16)
    half = pl.BlockSpec((tm, W), lambda i: (i, 0))
    full = pl.BlockSpec((tm, D), lambda i: (i, 0))
    wspec = pl.BlockSpec((W, D), lambda i: (0, 0))
    return pl.pallas_call(
        _outproj_body,
        grid=(T // tm,),
        in_specs=[half, half, full, wspec, wspec, pl.BlockSpec((1, D), lambda i: (0, 0))],
        out_specs=[full, full],
        out_shape=[jax.ShapeDtypeStruct((T, D), F32)] * 2,
        compiler_params=_params("parallel"),
        name="outproj",
    )(mix_r, mix_f, x2, wr, wf, g2.reshape(1, D))


def _peer_q_body(x_ref, wh_ref, wl_ref, q_ref):
    xh, xl = _split(x_ref[...], 2)
    q_ref[...] = (jnp.dot(xl, wh_ref[...], preferred_element_type=F32)
                  + jnp.dot(xh, wl_ref[...], preferred_element_type=F32)
                  + jnp.dot(xh, wh_ref[...], preferred_element_type=F32))


def _peer_q(xn, w_q):
    T, D = xn.shape
    Q = w_q.shape[1]
    tm = min(512, T)
    wh = w_q.astype(BF16)
    wl = (w_q - wh.astype(F32)).astype(BF16)
    wspec = pl.BlockSpec((D, Q), lambda i: (0, 0))
    return pl.pallas_call(
        _peer_q_body,
        grid=(T // tm,),
        in_specs=[pl.BlockSpec((tm, D), lambda i: (i, 0)), wspec, wspec],
        out_specs=pl.BlockSpec((tm, Q), lambda i: (i, 0)),
        out_shape=jax.ShapeDtypeStruct((T, Q), F32),
        compiler_params=_params("parallel"),
        name="peer_q",
    )(xn, wh, wl)


def _candidate_pairs(k):
    return [(i, j) for i in range(k) for j in range(k) if (i + 1) * (j + 1) <= k]


def _peer_scores_body(q_ref, kh_ref, kl_ref, thr_ref, e1_ref, s2_ref, e2_ref, top_ref, cnt_ref, cand_ref, mult_ref,
                      *, nkeys, topk):
    pairs = _candidate_pairs(topk)
    npad = cand_ref.shape[0]
    tm = q_ref.shape[0]
    for h in range(kh_ref.shape[0]):
        q = q_ref[:, h * LANES:(h + 1) * LANES]
        qh, ql = _split(q, 2)
        khi, klo = kh_ref[h], kl_ref[h]
        st = (lax.dot_general(klo, qh, NT, preferred_element_type=F32)
              + lax.dot_general(khi, ql, NT, preferred_element_type=F32)
              + lax.dot_general(khi, qh, NT, preferred_element_type=F32))
        halves = (st[:nkeys], st[nkeys:])
        for p, cur in enumerate(halves):
            for i in range(topk):
                m = jnp.max(cur, axis=0, keepdims=True)
                eq = cur == m
                cnt = jnp.sum(jnp.where(eq, 1.0, 0.0), axis=0, keepdims=True)
                cur = jnp.where(eq, -jnp.inf, cur)
                top_ref[p, i:i + 1, :] = m
                cnt_ref[p, i:i + 1, :] = jnp.where(m == -jnp.inf, 0.0, cnt)
        cand_ref[...] = jnp.full_like(cand_ref, -jnp.inf)
        mult_ref[...] = jnp.zeros_like(mult_ref)
        for n, (i, j) in enumerate(pairs):
            cand_ref[n:n + 1, :] = top_ref[0, i:i + 1, :] + top_ref[1, j:j + 1, :]
            mult_ref[n:n + 1, :] = cnt_ref[0, i:i + 1, :] * cnt_ref[1, j:j + 1, :]
        cand, mult = cand_ref[...], mult_ref[...]
        rank = jnp.zeros((npad, tm), F32)
        for n in range(len(pairs)):
            rank = rank + jnp.where(cand_ref[n:n + 1, :] >= cand, mult_ref[n:n + 1, :], 0.0)
        tau = jnp.max(jnp.where(rank >= float(topk), cand, -jnp.inf), axis=0, keepdims=True)
        a1, b1 = top_ref[0, 0:1, :], top_ref[1, 0:1, :]
        z = jnp.sum(jnp.where(cand >= tau, mult * jnp.exp(cand - (a1 + b1)), 0.0), axis=0, keepdims=True)
        s1, s2 = halves
        thr_ref[h] = tau - s1
        e1_ref[h] = jnp.exp(s1 - a1) / z
        s2_ref[h] = s2
        e2_ref[h] = jnp.exp(s2 - b1)


def _peer_scores(q, sub_keys):
    T = q.shape[0]
    nh, _, nkeys, half = sub_keys.shape
    assert 2 * half == LANES
    tm = min(256, T)
    kb = jnp.zeros((nh, 2 * nkeys, LANES), F32)
    kb = kb.at[:, :nkeys, :half].set(sub_keys[:, 0]).at[:, nkeys:, half:].set(sub_keys[:, 1])
    kh = kb.astype(BF16)
    kl = (kb - kh.astype(F32)).astype(BF16)
    npad = -(-len(_candidate_pairs(PEER_TOPK)) // 8) * 8
    kspec = pl.BlockSpec((nh, 2 * nkeys, LANES), lambda i: (0, 0, 0))
    ospec = pl.BlockSpec((nh, nkeys, tm), lambda i: (0, 0, i))
    return pl.pallas_call(
        functools.partial(_peer_scores_body, nkeys=nkeys, topk=PEER_TOPK),
        grid=(T // tm,),
        in_specs=[pl.BlockSpec((tm, nh * LANES), lambda i: (i, 0)), kspec, kspec],
        out_specs=[ospec] * 4,
        out_shape=[jax.ShapeDtypeStruct((nh, nkeys, T), F32)] * 4,
        scratch_shapes=[pltpu.VMEM((2, PEER_TOPK, tm), F32), pltpu.VMEM((2, PEER_TOPK, tm), F32),
                        pltpu.VMEM((npad, tm), F32), pltpu.VMEM((npad, tm), F32)],
        compiler_params=_params("parallel"),
        name="peer_scores",
    )(q, kh, kl)


def _gelu(x):
    return 0.5 * x * (1.0 + lax.erf(x * (2.0 ** -0.5)))


def _peer_dense_body(x_ref, h_ref, u_ref, v_ref, thr_ref, e1_ref, s2_ref, e2_ref, o_ref, g_ref, *, nkeys, gi):
    e = pl.program_id(1)

    @pl.when(e == 0)
    def _():
        o_ref[...] = h_ref[...]

    x = x_ref[...]
    nh = s2_ref.shape[0]
    for j in range(gi):
        rows = slice(j * nkeys, (j + 1) * nkeys)
        hid = lax.dot_general(u_ref[rows, :], x, NT, preferred_element_type=F32)
        w = None
        for h in range(nh):
            t = jnp.where(s2_ref[h] >= thr_ref[h, j:j + 1, :], e2_ref[h], 0.0) * e1_ref[h, j:j + 1, :]
            w = t if w is None else w + t
        g_ref[rows, :] = (w * _gelu(hid)).astype(BF16)
    o_ref[...] += lax.dot_general(g_ref[...], v_ref[...], TN, preferred_element_type=F32)


def _peer_dense(xn_bf, h, u_bf, v_bf, thr, e1, s2, e2):
    T, D = h.shape
    E = u_bf.shape[0]
    nh, nkeys, _ = thr.shape
    tm = min(512, T)
    gi = 8
    te = gi * nkeys
    tok = pl.BlockSpec((tm, D), lambda i, e: (i, 0))
    tab = pl.BlockSpec((te, D), lambda i, e: (e, 0))
    per_i1 = pl.BlockSpec((nh, gi, tm), lambda i, e: (0, e, i))
    per_i2 = pl.BlockSpec((nh, nkeys, tm), lambda i, e: (0, 0, i))
    return pl.pallas_call(
        functools.partial(_peer_dense_body, nkeys=nkeys, gi=gi),
        grid=(T // tm, E // te),
        in_specs=[tok, tok, tab, tab, per_i1, per_i1, per_i2, per_i2],
        out_specs=tok,
        out_shape=jax.ShapeDtypeStruct((T, D), F32),
        scratch_shapes=[pltpu.VMEM((te, tm), BF16)],
        compiler_params=_params("parallel", "arbitrary"),
        name="peer_dense",
    )(xn_bf, h, u_bf, v_bf, thr, e1, s2, e2)


def _layer(h, B, S, norm1_g, w_in, mu_shift, w_decay_up, w0, w_iclr_up, a0, w_gate_up, k_k, k_a, r_k, lnx_g, lnx_b,
           fox_b_f, q_norm_g, k_norm_g, w_out, norm2_g, peer_w_q, peer_sub_keys, peer_u, peer_v):
    T, D = h.shape
    RW = w0.shape[0]
    nfh = fox_b_f.shape[0]
    FW = nfh * HEAD_DIM
    rwkv_cols = 3 * RW + w_decay_up.shape[0] + w_iclr_up.shape[0] + w_gate_up.shape[0]
    assert w_in.shape[1] == rwkv_cols + 3 * FW + nfh and FW == RW

    w_pad = jnp.concatenate([w_in[:, rwkv_cols:rwkv_cols + 3 * FW], w_in[:, :rwkv_cols],
                             w_in[:, rwkv_cols + 3 * FW:], jnp.zeros((D, LANES - nfh), F32)], axis=1).astype(BF16)
    proj = _inproj(h, norm1_g, w_pad)

    r, lw, k, v, a, b, g = _rwkv_prep(proj, S, 3 * FW, mu_shift, w_decay_up, w0, w_iclr_up, a0, w_gate_up, k_k, k_a)
    y = _rwkv_scan(r, lw, k, v, a, b, B, S)
    mix_r = _rwkv_post(y, r, k, v, g, lnx_g, lnx_b, r_k.reshape(-1))

    qn, kn, c = _fox_prep(proj, S, q_norm_g, k_norm_g, fox_b_f, FW, 3 * FW + rwkv_cols)
    heads = lambda t: t.reshape(B, S, nfh, HEAD_DIM).transpose(0, 2, 1, 3)
    fv = proj[:, 2 * FW:3 * FW].astype(BF16)
    o = _fox_attn(heads(qn), heads(kn), heads(fv), c[:, :nfh].reshape(B, S, nfh).transpose(0, 2, 1))
    mix_f = o.transpose(0, 2, 1, 3).reshape(T, FW)

    h2, xn2 = _outproj(mix_r, mix_f, h, w_out, norm2_g)

    q = _peer_q(xn2, peer_w_q)
    thr, e1, s2, e2 = _peer_scores(q, peer_sub_keys)
    return _peer_dense(xn2.astype(BF16), h2, peer_u.astype(BF16), peer_v.astype(BF16), thr, e1, s2, e2)


def kernel(x, norm1_g, w_in, mu_shift, w_decay_up, w0, w_iclr_up, a0, w_gate_up, k_k, k_a, r_k, lnx_g, lnx_b,
           fox_b_f, q_norm_g, k_norm_g, w_out, norm2_g, peer_w_q, peer_sub_keys, peer_u, peer_v):
    B, S, D = x.shape
    h = x.reshape(B * S, D)
    params = (norm1_g, w_in, mu_shift, w_decay_up, w0, w_iclr_up, a0, w_gate_up, k_k, k_a, r_k, lnx_g, lnx_b,
              fox_b_f, q_norm_g, k_norm_g, w_out, norm2_g, peer_w_q, peer_sub_keys, peer_u, peer_v)
    for l in range(norm1_g.shape[0]):
        h = _layer(h, B, S, *(p[l] for p in params))
    return h.reshape(B, S, D)
```

```python
import functools

import jax
import jax.numpy as jnp
import numpy as np
from jax import lax
from jax.experimental import pallas as pl
from jax.experimental.pallas import tpu as pltpu

F32 = jnp.float32
BF16 = jnp.bfloat16

HEAD_DIM = 64
LANES = 128
RMS_EPS = 1e-6
LNX_EPS = 64e-5
PEER_TOPK = 16
NEG = -1e30
LOG2E = 1.4426950408889634
VMEM_LIMIT = 56 * 1024 * 1024

NN = (((1,), (0,)), ((), ()))
NT = (((1,), (1,)), ((), ()))
TN = (((0,), (0,)), ((), ()))


def _params(*sem):
    return pltpu.CompilerParams(dimension_semantics=sem, vmem_limit_bytes=VMEM_LIMIT)


def _split(x, n):
    if x.dtype == BF16:
        return [x]
    pieces, rem = [], x
    for i in range(n):
        p = rem.astype(BF16)
        pieces.append(p)
        if i + 1 < n:
            rem = rem - p.astype(F32)
    return pieces


def _mm(a, b, dims=NN, pa=1, pb=1):
    A, B = _split(a, pa), _split(b, pb)
    order = max(len(A), len(B))
    out = None
    for i in reversed(range(len(A))):
        for j in reversed(range(len(B))):
            if i + j >= order:
                continue
            t = lax.dot_general(A[i], B[j], dims, preferred_element_type=F32)
            out = t if out is None else out + t
    return out


def _inproj_body(x_ref, g_ref, w_ref, o_ref, xn_ref):
    @pl.when(pl.program_id(1) == 0)
    def _():
        xf = x_ref[...]
        ms = jnp.mean(xf * xf, axis=-1, keepdims=True)
        xn_ref[...] = (xf * lax.rsqrt(ms + RMS_EPS) * g_ref[...]).astype(BF16)

    o_ref[...] = jnp.dot(xn_ref[...], w_ref[...], preferred_element_type=F32)


def _inproj(x2, g, w_pad):
    T, D = x2.shape
    NC = w_pad.shape[1]
    tm = min(512, T)
    tn = NC // 3
    return pl.pallas_call(
        _inproj_body,
        grid=(T // tm, NC // tn),
        in_specs=[pl.BlockSpec((tm, D), lambda i, j: (i, 0)),
                  pl.BlockSpec((1, D), lambda i, j: (0, 0)),
                  pl.BlockSpec((D, tn), lambda i, j: (0, j))],
        out_specs=pl.BlockSpec((tm, tn), lambda i, j: (i, j)),
        out_shape=jax.ShapeDtypeStruct((T, NC), F32),
        scratch_shapes=[pltpu.VMEM((tm, D), BF16)],
        compiler_params=_params("parallel", "arbitrary"),
        name="inproj",
    )(x2, g.reshape(1, D), w_pad)


def _softplus(z):
    return jnp.maximum(z, 0.0) + jnp.log1p(jnp.exp(-jnp.abs(z)))


def _rwkv_prep_body(r_ref, rp_ref, k_ref, kp_ref, v_ref, vp_ref, lo_ref, lop_ref, gd_ref, gdp_ref,
                    mur_ref, muk_ref, muv_ref, mulo_ref, mugd_ref,
                    wdec_ref, w0_ref, wicl_ref, a0_ref, wgate_ref, kk_ref, ka_ref, gsum_ref,
                    ro_ref, lwo_ref, ko_ref, vo_ref, ao_ref, bo_ref, go_ref, *, tiles_per_seq):
    first = (pl.program_id(0) % tiles_per_seq) == 0

    def shift(cur_ref, prev_ref, mu_ref):
        cur = cur_ref[...]
        rolled = pltpu.roll(cur, 1, 0)
        prow = jnp.where(first, 0.0, prev_ref[7:8, :])
        row = lax.broadcasted_iota(jnp.int32, cur.shape, 0)
        prev = jnp.where(row == 0, prow, rolled)
        return cur + mu_ref[...] * (prev - cur)

    r = shift(r_ref, rp_ref, mur_ref)
    k = shift(k_ref, kp_ref, muk_ref)
    v = shift(v_ref, vp_ref, muv_ref)
    lo = shift(lo_ref, lop_ref, mulo_ref)
    gd = shift(gd_ref, gdp_ref, mugd_ref)

    w_raw = -_softplus(-(w0_ref[...] + _mm(jnp.tanh(lo), wdec_ref[...], pa=2, pb=2))) - 0.5
    lwo_ref[...] = -jnp.exp(w_raw)
    a = jax.nn.sigmoid(a0_ref[...] + _mm(lo, wicl_ref[...], pa=2, pb=2))
    go_ref[...] = _mm(jax.nn.sigmoid(gd), wgate_ref[...], pa=2, pb=2)
    kk = k * kk_ref[...]
    gsum = gsum_ref[...]
    for j in range(kk.shape[1] // LANES):
        sl = slice(j * LANES, (j + 1) * LANES)
        kkj = kk[:, sl]
        ss = _mm(kkj * kkj, gsum, pa=3, pb=1)
        kkn = kkj / jnp.maximum(jnp.sqrt(ss), 1e-12)
        ao_ref[:, sl] = -kkn
        bo_ref[:, sl] = kkn * a[:, sl]
    ro_ref[...] = r
    vo_ref[...] = v
    ko_ref[...] = k * (1.0 + (a - 1.0) * ka_ref[...])


def _rwkv_prep(proj, S, col0, mu, w_decay_up, w0, w_iclr_up, a0, w_gate_up, k_k, k_a):
    T = proj.shape[0]
    W = w0.shape[0]
    dl, il = w_decay_up.shape[0], w_iclr_up.shape[0]
    assert dl + il == LANES and w_gate_up.shape[0] == LANES and W % LANES == 0 and col0 % W == 0
    tm = min(256, S)
    cb = col0 // W
    lb = (col0 + 3 * W) // LANES
    nprev = tm // 8

    def cur(width, c):
        return pl.BlockSpec((tm, width), lambda i: (i, c))

    def prev(width, c):
        return pl.BlockSpec((8, width), lambda i: (jnp.maximum(i * nprev - 1, 0), c))

    def full(shape):
        return pl.BlockSpec(shape, lambda i: (0,) * len(shape))

    row = lambda a: a.reshape(1, -1)
    wdec = jnp.concatenate([w_decay_up, jnp.zeros((il, W), F32)], axis=0)
    wicl = jnp.concatenate([jnp.zeros((dl, W), F32), w_iclr_up], axis=0)
    gsum = jnp.asarray(np.kron(np.eye(LANES // HEAD_DIM), np.ones((HEAD_DIM, HEAD_DIM))), F32)
    o = np.cumsum([W, W, W, dl + il])
    outs = pl.pallas_call(
        functools.partial(_rwkv_prep_body, tiles_per_seq=S // tm),
        grid=(T // tm,),
        in_specs=[cur(W, cb), prev(W, cb), cur(W, cb + 1), prev(W, cb + 1), cur(W, cb + 2), prev(W, cb + 2),
                  cur(LANES, lb), prev(LANES, lb), cur(LANES, lb + 1), prev(LANES, lb + 1),
                  full((1, W)), full((1, W)), full((1, W)), full((1, LANES)), full((1, LANES)),
                  full((LANES, W)), full((1, W)), full((LANES, W)), full((1, W)), full((LANES, W)),
                  full((1, W)), full((1, W)), full((LANES, LANES))],
        out_specs=[pl.BlockSpec((tm, W), lambda i: (i, 0))] * 7,
        out_shape=[jax.ShapeDtypeStruct((T, W), F32)] * 7,
        compiler_params=_params("parallel"),
        name="rwkv_prep",
    )(proj, proj, proj, proj, proj, proj, proj, proj, proj, proj,
      row(mu[:o[0]]), row(mu[o[0]:o[1]]), row(mu[o[1]:o[2]]), row(mu[o[2]:o[3]]), row(mu[o[3]:]),
      wdec, row(w0), wicl, row(a0), w_gate_up, row(k_k), row(k_a), gsum)
    return outs


def _scan_body(r_ref, lw_ref, k_ref, v_ref, a_ref, b_ref, y_ref, s_ref, *, C, LC):
    @pl.when(pl.program_id(2) == 0)
    def _():
        s_ref[...] = jnp.zeros_like(s_ref)

    lane = lax.broadcasted_iota(jnp.int32, (1, LANES), 1)
    head_a = lane < HEAD_DIM
    m_a = head_a.astype(F32)
    m_b = 1.0 - m_a
    ri = lax.broadcasted_iota(jnp.int32, (C, C), 0)
    ci = lax.broadcasted_iota(jnp.int32, (C, C), 1)
    tri_incl = ri >= ci
    tri_strict = ri > ci
    tri_f = tri_incl.astype(F32)
    eye = (ri == ci).astype(F32)
    bi = lax.broadcasted_iota(jnp.int32, (LANES, LANES), 0) // HEAD_DIM
    bj = lax.broadcasted_iota(jnp.int32, (LANES, LANES), 1) // HEAD_DIM
    bd = (bi == bj).astype(F32)

    mm = functools.partial(_mm, pa=2, pb=2)
    sel = lambda xa, xb: jnp.where(head_a, xa, xb)

    S = s_ref[...]
    for i in range(LC):
        sl = slice(i * C, (i + 1) * C)
        r, lw, k, v, a, b = (ref[0, sl, :] for ref in (r_ref, lw_ref, k_ref, v_ref, a_ref, b_ref))
        cum = _mm(tri_f, lw, pa=1, pb=3)
        p_in = jnp.exp(cum)
        p_inv = jnp.exp(-cum)
        at = a * jnp.exp(cum - lw)
        bt = b * p_inv
        kt = k * p_inv
        rt = r * p_in
        cum_end = cum[C - 1:C, :]
        p_end = jnp.exp(cum_end)
        to_end = jnp.exp(cum_end - cum)
        bh = b * to_end
        kh = k * to_end

        def per_head(m):
            atm, rtm = at * m, rt * m
            a_ab = jnp.where(tri_strict, mm(atm, bt, NT), 0.0)
            a_ak = jnp.where(tri_strict, mm(atm, kt, NT), 0.0)
            a_rb = jnp.where(tri_incl, mm(rtm, bt, NT), 0.0)
            a_rk = jnp.where(tri_incl, mm(rtm, kt, NT), 0.0)
            tinv, pw = eye + a_ab, a_ab
            n = 2
            while n < C:
                pw = mm(pw, pw)
                tinv = tinv + mm(tinv, pw)
                n *= 2
            return a_ak, a_rb, a_rk, tinv

        ak_a, rb_a, rk_a, t_a = per_head(m_a)
        ak_b, rb_b, rk_b, t_b = per_head(m_b)
        x = mm(at, S, NT) + sel(mm(ak_a, v), mm(ak_b, v))
        u = sel(mm(t_a, x), mm(t_b, x))
        y = mm(rt, S, NT) + sel(mm(rb_a, u) + mm(rk_a, v), mm(rb_b, u) + mm(rk_b, v))
        y_ref[0, sl, :] = y
        S = S * p_end + bd * (mm(u, bh, TN) + mm(v, kh, TN))
    s_ref[...] = S


def _rwkv_scan(r, lw, k, v, a, b, B, S):
    W = r.shape[-1]
    C = min(64, S)
    LC = max(1, min(4, S // C))
    R = C * LC
    spec = pl.BlockSpec((1, R, LANES), lambda bi, h, c: (bi, c, h))
    args = [t.reshape(B, S, W) for t in (r, lw, k, v, a, b)]
    y = pl.pallas_call(
        functools.partial(_scan_body, C=C, LC=LC),
        grid=(B, W // LANES, S // R),
        in_specs=[spec] * 6,
        out_specs=spec,
        out_shape=jax.ShapeDtypeStruct((B, S, W), F32),
        scratch_shapes=[pltpu.VMEM((LANES, LANES), F32)],
        compiler_params=_params("parallel", "parallel", "arbitrary"),
        name="rwkv_scan",
    )(*args)
    return y.reshape(B * S, W)


def _rwkv_post_body(y_ref, r_ref, k_ref, v_ref, g_ref, lg_ref, lb_ref, rk_ref, gmean_ref, gsum_ref, o_ref):
    gmean, gsum = gmean_ref[...], gsum_ref[...]
    for j in range(y_ref.shape[1] // LANES):
        sl = slice(j * LANES, (j + 1) * LANES)
        y = y_ref[:, sl]
        mean = _mm(y, gmean, pa=3, pb=1)
        d = y - mean
        var = _mm(d * d, gmean, pa=3, pb=1)
        yn = d * lax.rsqrt(var + LNX_EPS) * lg_ref[:, sl] + lb_ref[:, sl]
        v = v_ref[:, sl]
        bonus = _mm(r_ref[:, sl] * k_ref[:, sl] * rk_ref[:, sl], gsum, pa=3, pb=1) * v
        o_ref[:, sl] = ((yn + bonus) * g_ref[:, sl]).astype(o_ref.dtype)


def _rwkv_post(y, r, k, v, g, lnx_g, lnx_b, r_k):
    T, W = y.shape
    tm = min(512, T)
    blk = np.kron(np.eye(LANES // HEAD_DIM), np.ones((HEAD_DIM, HEAD_DIM)))
    tile = pl.BlockSpec((tm, W), lambda i: (i, 0))
    rowspec = pl.BlockSpec((1, W), lambda i: (0, 0))
    sq = pl.BlockSpec((LANES, LANES), lambda i: (0, 0))
    return pl.pallas_call(
        _rwkv_post_body,
        grid=(T // tm,),
        in_specs=[tile] * 5 + [rowspec] * 3 + [sq] * 2,
        out_specs=tile,
        out_shape=jax.ShapeDtypeStruct((T, W), BF16),
        compiler_params=_params("parallel"),
        name="rwkv_post",
    )(y, r, k, v, g, lnx_g.reshape(1, W), lnx_b.reshape(1, W), r_k.reshape(1, W),
      jnp.asarray(blk / HEAD_DIM, F32), jnp.asarray(blk, F32))


N_AUX = 3


def _fox_prep_body(q_ref, k_ref, v_ref, f_ref, qg_ref, kg_ref, fb_ref, gmean_ref, sel_ref,
                   qo_ref, ko_ref, vo_ref, c_ref, carry_ref, *, tiles_per_seq, scale):
    @pl.when(pl.program_id(0) % tiles_per_seq == 0)
    def _():
        carry_ref[...] = jnp.zeros_like(carry_ref)

    log_f = jax.nn.log_sigmoid(f_ref[...] + fb_ref[...])
    tm = log_f.shape[0]
    ri = lax.broadcasted_iota(jnp.int32, (tm, tm), 0)
    ci = lax.broadcasted_iota(jnp.int32, (tm, tm), 1)
    c = _mm((ri >= ci).astype(F32), log_f, pa=1, pb=3) + carry_ref[...]
    carry_ref[...] = c[tm - 1:tm, :]
    c2 = c * LOG2E
    c_ref[...] = c2
    pieces = jnp.concatenate(_split(c2, N_AUX), axis=1)
    lane = lax.broadcasted_iota(jnp.int32, (tm, LANES), 1)
    q_aux = jnp.where(lane < 2 * N_AUX, -1.0, 0.0).astype(BF16)

    gmean = gmean_ref[...]
    for p in range(q_ref.shape[1] // LANES):
        sl = slice(p * LANES, (p + 1) * LANES)
        main = slice(2 * p * LANES, (2 * p + 1) * LANES)
        aux = slice((2 * p + 1) * LANES, (2 * p + 2) * LANES)
        for src, gain, dst, mul in ((q_ref, qg_ref, qo_ref, scale), (k_ref, kg_ref, ko_ref, 1.0)):
            t = src[:, sl]
            ms = _mm(t * t, gmean, pa=3, pb=1)
            dst[:, main] = (t * lax.rsqrt(ms + RMS_EPS) * gain[...] * mul).astype(BF16)
        qo_ref[:, aux] = q_aux
        ko_ref[:, aux] = jnp.dot(pieces, sel_ref[p], preferred_element_type=F32).astype(BF16)
    vo_ref[...] = v_ref[...].astype(BF16)


def _fox_prep(proj, S, q_norm_g, k_norm_g, fox_b_f, W, f_col):
    T = proj.shape[0]
    tm = min(256, S)
    nh = fox_b_f.shape[0]
    npairs = W // LANES
    gq = jnp.tile(q_norm_g, LANES // HEAD_DIM).reshape(1, LANES)
    gk = jnp.tile(k_norm_g, LANES // HEAD_DIM).reshape(1, LANES)
    fb = jnp.concatenate([fox_b_f, jnp.zeros((LANES - nh,), F32)]).reshape(1, LANES)
    blk = np.kron(np.eye(LANES // HEAD_DIM), np.ones((HEAD_DIM, HEAD_DIM))) / HEAD_DIM
    sel = np.zeros((npairs, N_AUX * LANES, LANES), np.float32)
    for p in range(npairs):
        for s in range(2):
            for n in range(N_AUX):
                sel[p, n * LANES + 2 * p + s, s * N_AUX + n] = 1.0
    small = pl.BlockSpec((1, LANES), lambda i: (0, 0))
    wide = pl.BlockSpec((tm, 2 * W), lambda i: (i, 0))
    return pl.pallas_call(
        functools.partial(_fox_prep_body, tiles_per_seq=S // tm, scale=HEAD_DIM ** -0.5 * LOG2E),
        grid=(T // tm,),
        in_specs=[pl.BlockSpec((tm, W), lambda i: (i, 0)), pl.BlockSpec((tm, W), lambda i: (i, 1)),
                  pl.BlockSpec((tm, W), lambda i: (i, 2)),
                  pl.BlockSpec((tm, LANES), lambda i: (i, f_col // LANES)),
                  small, small, small, pl.BlockSpec((LANES, LANES), lambda i: (0, 0)),
                  pl.BlockSpec((npairs, N_AUX * LANES, LANES), lambda i: (0, 0, 0))],
        out_specs=[wide, wide, pl.BlockSpec((tm, W), lambda i: (i, 0)), pl.BlockSpec((tm, LANES), lambda i: (i, 0))],
        out_shape=[jax.ShapeDtypeStruct((T, 2 * W), BF16), jax.ShapeDtypeStruct((T, 2 * W), BF16),
                   jax.ShapeDtypeStruct((T, W), BF16), jax.ShapeDtypeStruct((T, LANES), F32)],
        scratch_shapes=[pltpu.VMEM((1, LANES), F32)],
        compiler_params=_params("arbitrary"),
        name="fox_prep",
    )(proj, proj, proj, proj, gq, gk, fb, jnp.asarray(blk, F32), jnp.asarray(sel, BF16))


def _attn_body(q_ref, k_ref, v_ref, c_ref, o_ref, *, t):
    pair, qi = pl.program_id(1), pl.program_id(2)
    lane = lax.broadcasted_iota(jnp.int32, (1, 2 * LANES), 1)
    in_a = (lane < HEAD_DIM) | ((lane >= LANES) & (lane < LANES + N_AUX))
    in_b = ((lane >= HEAD_DIM) & (lane < LANES)) | ((lane >= LANES + N_AUX) & (lane < LANES + 2 * N_AUX))
    q = q_ref[...]
    q_a = q * jnp.where(in_a, 1.0, 0.0).astype(BF16)
    q_b = q * jnp.where(in_b, 1.0, 0.0).astype(BF16)
    c_t = c_ref[...].T
    row = lax.broadcasted_iota(jnp.int32, (LANES, 1), 0)
    cq_a = jnp.sum(jnp.where(row == 2 * pair, c_t, 0.0), axis=0, keepdims=True)
    cq_b = jnp.sum(jnp.where(row == 2 * pair + 1, c_t, 0.0), axis=0, keepdims=True)
    top = row < HEAD_DIM

    def head(s, cq, m_old, l_old, v_t):
        m_new = jnp.maximum(m_old, jnp.max(s, axis=0, keepdims=True) + cq)
        p = jnp.exp2(s - (m_new - cq))
        alpha = jnp.exp2(m_old - m_new)
        l_new = alpha * l_old + jnp.sum(p, axis=0, keepdims=True)
        return m_new, l_new, alpha, jnp.dot(v_t, p.astype(BF16), preferred_element_type=F32)

    def tile(j, carry, diagonal):
        m_a, l_a, m_b, l_b, acc = carry
        k = k_ref[pl.ds(pl.multiple_of(j * t, t), t), :]
        v_t = v_ref[0, 0, j]
        s_a = lax.dot_general(k, q_a, NT, preferred_element_type=F32)
        s_b = lax.dot_general(k, q_b, NT, preferred_element_type=F32)
        if diagonal:
            keep = (lax.broadcasted_iota(jnp.int32, (t, t), 0) <= lax.broadcasted_iota(jnp.int32, (t, t), 1))
            s_a = jnp.where(keep, s_a, NEG)
            s_b = jnp.where(keep, s_b, NEG)
        m_a, l_a, al_a, o_a = head(s_a, cq_a, m_a, l_a, v_t)
        m_b, l_b, al_b, o_b = head(s_b, cq_b, m_b, l_b, v_t)
        acc = jnp.where(top, al_a, al_b) * acc + jnp.where(top, o_a, o_b)
        return m_a, l_a, m_b, l_b, acc

    stat = lambda val: jnp.full((1, t), val, F32)
    init = (stat(NEG), stat(0.0), stat(NEG), stat(0.0), jnp.zeros((LANES, t), F32))
    carry = lax.fori_loop(0, qi, lambda j, cr: tile(j, cr, False), init)
    _, l_a, _, l_b, acc = tile(qi, carry, True)
    o_ref[...] = (acc / jnp.where(top, l_a, l_b)).T.astype(o_ref.dtype)


def _fox_attn(q_aug, k_aug, v_bf, c2, B, S):
    T, W = v_bf.shape
    npairs = W // LANES
    t = min(512, S)
    n = S // t
    v_t = v_bf.reshape(B, n, t, npairs, LANES).transpose(0, 3, 1, 4, 2)
    return pl.pallas_call(
        functools.partial(_attn_body, t=t),
        grid=(B, npairs, n),
        in_specs=[pl.BlockSpec((t, 2 * LANES), lambda b, p, i: (b * n + i, p)),
                  pl.BlockSpec((S, 2 * LANES), lambda b, p, i: (b, p)),
                  pl.BlockSpec((1, 1, n, LANES, t), lambda b, p, i: (b, p, 0, 0, 0)),
                  pl.BlockSpec((t, LANES), lambda b, p, i: (b * n + i, 0))],
        out_specs=pl.BlockSpec((t, LANES), lambda b, p, i: (b * n + i, p)),
        out_shape=jax.ShapeDtypeStruct((T, W), BF16),
        compiler_params=_params("parallel", "parallel", "arbitrary"),
        name="fox_attn",
    )(q_aug, k_aug, v_t, c2)


def _outproj_body(mr_ref, mf_ref, x_ref, wr_ref, wf_ref, g_ref, h_ref, xn_ref):
    h = (x_ref[...] + jnp.dot(mr_ref[...], wr_ref[...], preferred_element_type=F32)
         + jnp.dot(mf_ref[...], wf_ref[...], preferred_element_type=F32))
    h_ref[...] = h
    ms = jnp.mean(h * h, axis=-1, keepdims=True)
    xn_ref[...] = h * lax.rsqrt(ms + RMS_EPS) * g_ref[...]


def _outproj(mix_r, mix_f, x2, w_out, g2):
    T, D = x2.shape
    W = mix_r.shape[1]
    tm = min(256, T)
    wr, wf = w_out[:W].astype(BF16), w_out[W:].astype(BF16)
    half = pl.BlockSpec((tm, W), lambda i: (i, 0))
    full = pl.BlockSpec((tm, D), lambda i: (i, 0))
    wspec = pl.BlockSpec((W, D), lambda i: (0, 0))
    return pl.pallas_call(
        _outproj_body,
        grid=(T // tm,),
        in_specs=[half, half, full, wspec, wspec, pl.BlockSpec((1, D), lambda i: (0, 0))],
        out_specs=[full, full],
        out_shape=[jax.ShapeDtypeStruct((T, D), F32)] * 2,
        compiler_params=_params("parallel"),
        name="outproj",
    )(mix_r, mix_f, x2, wr, wf, g2.reshape(1, D))


def _peer_q_body(x_ref, wh_ref, wl_ref, q_ref):
    xh, xl = _split(x_ref[...], 2)
    q_ref[...] = (jnp.dot(xl, wh_ref[...], preferred_element_type=F32)
                  + jnp.dot(xh, wl_ref[...], preferred_element_type=F32)
                  + jnp.dot(xh, wh_ref[...], preferred_element_type=F32))


def _peer_q(xn, w_q):
    T, D = xn.shape
    Q = w_q.shape[1]
    tm = min(512, T)
    wh = w_q.astype(BF16)
    wl = (w_q - wh.astype(F32)).astype(BF16)
    wspec = pl.BlockSpec((D, Q), lambda i: (0, 0))
    return pl.pallas_call(
        _peer_q_body,
        grid=(T // tm,),
        in_specs=[pl.BlockSpec((tm, D), lambda i: (i, 0)), wspec, wspec],
        out_specs=pl.BlockSpec((tm, Q), lambda i: (i, 0)),
        out_shape=jax.ShapeDtypeStruct((T, Q), F32),
        compiler_params=_params("parallel"),
        name="peer_q",
    )(xn, wh, wl)


def _candidate_pairs(k):
    return [(i, j) for i in range(k) for j in range(k) if (i + 1) * (j + 1) <= k]


def _peer_scores_body(q_ref, kh_ref, kl_ref, thr_ref, e1_ref, s2_ref, e2_ref, top_ref, cnt_ref, cand_ref, mult_ref,
                      *, nkeys, topk):
    pairs = _candidate_pairs(topk)
    npad = cand_ref.shape[0]
    tm = q_ref.shape[0]
    for h in range(kh_ref.shape[0]):
        q = q_ref[:, h * LANES:(h + 1) * LANES]
        qh, ql = _split(q, 2)
        khi, klo = kh_ref[h], kl_ref[h]
        st = (lax.dot_general(klo, qh, NT, preferred_element_type=F32)
              + lax.dot_general(khi, ql, NT, preferred_element_type=F32)
              + lax.dot_general(khi, qh, NT, preferred_element_type=F32))
        halves = (st[:nkeys], st[nkeys:])
        for p, cur in enumerate(halves):
            for i in range(topk):
                m = jnp.max(cur, axis=0, keepdims=True)
                eq = cur == m
                cnt = jnp.sum(jnp.where(eq, 1.0, 0.0), axis=0, keepdims=True)
                cur = jnp.where(eq, -jnp.inf, cur)
                top_ref[p, i:i + 1, :] = m
                cnt_ref[p, i:i + 1, :] = jnp.where(m == -jnp.inf, 0.0, cnt)
        cand_ref[...] = jnp.full_like(cand_ref, -jnp.inf)
        mult_ref[...] = jnp.zeros_like(mult_ref)
        for n, (i, j) in enumerate(pairs):
            cand_ref[n:n + 1, :] = top_ref[0, i:i + 1, :] + top_ref[1, j:j + 1, :]
            mult_ref[n:n + 1, :] = cnt_ref[0, i:i + 1, :] * cnt_ref[1, j:j + 1, :]
        cand, mult = cand_ref[...], mult_ref[...]
        rank = jnp.zeros((npad, tm), F32)
        for n in range(len(pairs)):
            rank = rank + jnp.where(cand_ref[n:n + 1, :] >= cand, mult_ref[n:n + 1, :], 0.0)
        tau = jnp.max(jnp.where(rank >= float(topk), cand, -jnp.inf), axis=0, keepdims=True)
        a1, b1 = top_ref[0, 0:1, :], top_ref[1, 0:1, :]
        z = jnp.sum(jnp.where(cand >= tau, mult * jnp.exp(cand - (a1 + b1)), 0.0), axis=0, keepdims=True)
        s1, s2 = halves
        thr_ref[h] = tau - s1
        e1_ref[h] = jnp.exp(s1 - a1) / z
        s2_ref[h] = s2
        e2_ref[h] = jnp.exp(s2 - b1)


def _peer_scores(q, sub_keys):
    T = q.shape[0]
    nh, _, nkeys, half = sub_keys.shape
    assert 2 * half == LANES
    tm = min(256, T)
    kb = jnp.zeros((nh, 2 * nkeys, LANES), F32)
    kb = kb.at[:, :nkeys, :half].set(sub_keys[:, 0]).at[:, nkeys:, half:].set(sub_keys[:, 1])
    kh = kb.astype(BF16)
    kl = (kb - kh.astype(F32)).astype(BF16)
    npad = -(-len(_candidate_pairs(PEER_TOPK)) // 8) * 8
    kspec = pl.BlockSpec((nh, 2 * nkeys, LANES), lambda i: (0, 0, 0))
    ospec = pl.BlockSpec((nh, nkeys, tm), lambda i: (0, 0, i))
    return pl.pallas_call(
        functools.partial(_peer_scores_body, nkeys=nkeys, topk=PEER_TOPK),
        grid=(T // tm,),
        in_specs=[pl.BlockSpec((tm, nh * LANES), lambda i: (i, 0)), kspec, kspec],
        out_specs=[ospec] * 4,
        out_shape=[jax.ShapeDtypeStruct((nh, nkeys, T), F32)] * 4,
        scratch_shapes=[pltpu.VMEM((2, PEER_TOPK, tm), F32), pltpu.VMEM((2, PEER_TOPK, tm), F32),
                        pltpu.VMEM((npad, tm), F32), pltpu.VMEM((npad, tm), F32)],
        compiler_params=_params("parallel"),
        name="peer_scores",
    )(q, kh, kl)


def _gelu(x):
    return 0.5 * x * (1.0 + lax.erf(x * (2.0 ** -0.5)))


def _peer_dense_body(x_ref, h_ref, u_ref, v_ref, thr_ref, e1_ref, s2_ref, e2_ref, o_ref, g_ref, *, nkeys, gi):
    e = pl.program_id(1)

    @pl.when(e == 0)
    def _():
        o_ref[...] = h_ref[...]

    x = x_ref[...]
    nh = s2_ref.shape[0]
    for j in range(gi):
        rows = slice(j * nkeys, (j + 1) * nkeys)
        hid = lax.dot_general(u_ref[rows, :], x, NT, preferred_element_type=F32)
        w = None
        for h in range(nh):
            t = jnp.where(s2_ref[h] >= thr_ref[h, j:j + 1, :], e2_ref[h], 0.0) * e1_ref[h, j:j + 1, :]
            w = t if w is None else w + t
        g_ref[rows, :] = (w * _gelu(hid)).astype(BF16)
    o_ref[...] += lax.dot_general(g_ref[...], v_ref[...], TN, preferred_element_type=F32)


def _peer_dense(xn_bf, h, u_bf, v_bf, thr, e1, s2, e2):
    T, D = h.shape
    E = u_bf.shape[0]
    nh, nkeys, _ = thr.shape
    tm = min(512, T)
    gi = 8
    te = gi * nkeys
    tok = pl.BlockSpec((tm, D), lambda i, e: (i, 0))
    tab = pl.BlockSpec((te, D), lambda i, e: (e, 0))
    per_i1 = pl.BlockSpec((nh, gi, tm), lambda i, e: (0, e, i))
    per_i2 = pl.BlockSpec((nh, nkeys, tm), lambda i, e: (0, 0, i))
    return pl.pallas_call(
        functools.partial(_peer_dense_body, nkeys=nkeys, gi=gi),
        grid=(T // tm, E // te),
        in_specs=[tok, tok, tab, tab, per_i1, per_i1, per_i2, per_i2],
        out_specs=tok,
        out_shape=jax.ShapeDtypeStruct((T, D), F32),
        scratch_shapes=[pltpu.VMEM((te, tm), BF16)],
        compiler_params=_params("parallel", "arbitrary"),
        name="peer_dense",
    )(xn_bf, h, u_bf, v_bf, thr, e1, s2, e2)


def _layer(h, B, S, norm1_g, w_in, mu_shift, w_decay_up, w0, w_iclr_up, a0, w_gate_up, k_k, k_a, r_k, lnx_g, lnx_b,
           fox_b_f, q_norm_g, k_norm_g, w_out, norm2_g, peer_w_q, peer_sub_keys, peer_u, peer_v):
    T, D = h.shape
    RW = w0.shape[0]
    nfh = fox_b_f.shape[0]
    FW = nfh * HEAD_DIM
    rwkv_cols = 3 * RW + w_decay_up.shape[0] + w_iclr_up.shape[0] + w_gate_up.shape[0]
    assert w_in.shape[1] == rwkv_cols + 3 * FW + nfh and FW == RW

    w_pad = jnp.concatenate([w_in[:, rwkv_cols:rwkv_cols + 3 * FW], w_in[:, :rwkv_cols],
                             w_in[:, rwkv_cols + 3 * FW:], jnp.zeros((D, LANES - nfh), F32)], axis=1).astype(BF16)
    proj = _inproj(h, norm1_g, w_pad)

    r, lw, k, v, a, b, g = _rwkv_prep(proj, S, 3 * FW, mu_shift, w_decay_up, w0, w_iclr_up, a0, w_gate_up, k_k, k_a)
    y = _rwkv_scan(r, lw, k, v, a, b, B, S)
    mix_r = _rwkv_post(y, r, k, v, g, lnx_g, lnx_b, r_k.reshape(-1))

    q_aug, k_aug, v_bf, c2 = _fox_prep(proj, S, q_norm_g, k_norm_g, fox_b_f, FW, 3 * FW + rwkv_cols)
    mix_f = _fox_attn(q_aug, k_aug, v_bf, c2, B, S)

    h2, xn2 = _outproj(mix_r, mix_f, h, w_out, norm2_g)

    q = _peer_q(xn2, peer_w_q)
    thr, e1, s2, e2 = _peer_scores(q, peer_sub_keys)
    return _peer_dense(xn2.astype(BF16), h2, peer_u.astype(BF16), peer_v.astype(BF16), thr, e1, s2, e2)


def kernel(x, norm1_g, w_in, mu_shift, w_decay_up, w0, w_iclr_up, a0, w_gate_up, k_k, k_a, r_k, lnx_g, lnx_b,
           fox_b_f, q_norm_g, k_norm_g, w_out, norm2_g, peer_w_q, peer_sub_keys, peer_u, peer_v):
    B, S, D = x.shape
    h = x.reshape(B * S, D)
    params = (norm1_g, w_in, mu_shift, w_decay_up, w0, w_iclr_up, a0, w_gate_up, k_k, k_a, r_k, lnx_g, lnx_b,
              fox_b_f, q_norm_g, k_norm_g, w_out, norm2_g, peer_w_q, peer_sub_keys, peer_u, peer_v)
    for l in range(norm1_g.shape[0]):
        h = _layer(h, B, S, *(p[l] for p in params))
    return h.reshape(B, S, D)
```
